```python
import math
import jax, jax.numpy as jnp
from jax import lax
import numpy as np

D_MODEL = 1024
BATCH = 8
SEQ = 4096
DEPTH = 4

GRID_W = 64
CTX_LEN = 256
N_MIXERS = 4
ATT_HEADS = 8
ATT_DK = 64
ATT_DV = 2 * ATT_DK
ATT_QK_WIDTH = ATT_HEADS * 2 * ATT_DK
ATT_WIDTH = ATT_HEADS * ATT_DV
Q_BLOCK = 128
ROPE_BASE = 10000.0
CONF_WIDTH = D_MODEL
CONF_KW = 31
SC_WIDTH = D_MODEL
SC_KW = 3
POOL_WIDTH = D_MODEL
POOL_WINDOWS = (2, 4, 8, 16)
POOL_GROUP = POOL_WIDTH // len(POOL_WINDOWS)
DEEPNORM_ALPHA = (2 * DEPTH) ** 0.25
DEEPNORM_BETA = (8 * DEPTH) ** -0.25
LN_EPS = 1e-5

kernel_name = "hybrid_interleaved_diffusion_block"


def _layers_of_kind(kind):
    return len(range(kind, DEPTH, N_MIXERS))


def layer_norm(x, g, b):
    xf = x.astype(jnp.float32)
    mu = jnp.mean(xf, axis=-1, keepdims=True)
    var = jnp.mean(jnp.square(xf - mu), axis=-1, keepdims=True)
    y = (xf - mu) * lax.rsqrt(var + LN_EPS)
    return (y * g.astype(jnp.float32) + b.astype(jnp.float32)).astype(x.dtype)


def rms_norm(x, g):
    xf = x.astype(jnp.float32)
    y = xf * lax.rsqrt(jnp.mean(jnp.square(xf), axis=-1, keepdims=True) + LN_EPS)
    return (y * g.astype(jnp.float32)).astype(x.dtype)


def axial_rope_tables(length, dim):
    half = dim // 2
    inv = 1.0 / (ROPE_BASE ** (jnp.arange(0, half, 2, dtype=jnp.float32) / half))
    t = jnp.arange(length, dtype=jnp.int32)
    row, col = jnp.divmod(t, GRID_W)
    ar = row.astype(jnp.float32)[:, None] * inv[None, :]
    ac = col.astype(jnp.float32)[:, None] * inv[None, :]
    ang = jnp.concatenate([ar, ar, ac, ac], axis=-1)
    return jnp.cos(ang), jnp.sin(ang)


def rotate_half_axial(x):
    x0, x1, x2, x3 = jnp.split(x, 4, axis=-1)
    return jnp.concatenate([-x1, x0, -x3, x2], axis=-1)


def apply_rope(x, cos, sin):
    cos = cos[:, None, None, :].astype(x.dtype)
    sin = sin[:, None, None, :].astype(x.dtype)
    return x * cos + rotate_half_axial(x) * sin


def depthwise_conv_centered(x, w, b=None):
    pad = (w.shape[0] - 1) // 2
    y = lax.conv_general_dilated(x, w[:, None, :], window_strides=(1,), padding=[(pad, pad)],
                                 dimension_numbers=('NWC', 'WIO', 'NWC'),
                                 feature_group_count=x.shape[-1])
    return y if b is None else y + b


def _diff_attend(qb, k_all, v_all, lam):
    s = jnp.einsum('bhmqd,bhmkd->bhmqk', qb, k_all).astype(jnp.float32) * (ATT_DK ** -0.5)
    p = jax.nn.softmax(s, axis=-1)
    a = (p[:, :, 0] - lam * p[:, :, 1]).astype(v_all.dtype)
    return jnp.einsum('bhqk,bhkd->bhqd', a, v_all)


def _diff_head_out(o, z, subln_g, lambda_init, w_out):
    o = rms_norm(o, subln_g) * (1.0 - lambda_init)
    o = o.reshape(o.shape[0], o.shape[1], ATT_WIDTH)
    return (o * jax.nn.silu(z)) @ w_out


def diff_attention_mixer(h, hc, w_in, w_out, lq1, lk1, lq2, lk2, subln_g, lambda_init, cos, sin, with_ctx_queries):
    B, L, _ = h.shape
    Lc = hc.shape[1]
    w_q, w_k, w_v, w_z = jnp.split(w_in, [ATT_QK_WIDTH, 2 * ATT_QK_WIDTH, 2 * ATT_QK_WIDTH + ATT_WIDTH], axis=1)
    q = apply_rope((h @ w_q).reshape(B, L, ATT_HEADS, 2, ATT_DK), cos, sin)
    k = apply_rope((h @ w_k).reshape(B, L, ATT_HEADS, 2, ATT_DK), cos, sin)
    v = (h @ w_v).reshape(B, L, ATT_HEADS, ATT_DV)
    z = h @ w_z
    kc = (hc @ w_k).reshape(B, Lc, ATT_HEADS, 2, ATT_DK)
    vc = (hc @ w_v).reshape(B, Lc, ATT_HEADS, ATT_DV)
    f32 = jnp.float32
    lam = (jnp.exp(jnp.sum(lq1.astype(f32) * lk1.astype(f32)))
           - jnp.exp(jnp.sum(lq2.astype(f32) * lk2.astype(f32))) + lambda_init)

    to_bhmld = lambda t: t.transpose(0, 2, 3, 1, 4)
    k_all = to_bhmld(jnp.concatenate([k, kc], axis=1))
    v_all = jnp.concatenate([v, vc], axis=1).transpose(0, 2, 1, 3)
    nblk = L // Q_BLOCK
    qb = jnp.moveaxis(to_bhmld(q).reshape(B, ATT_HEADS, 2, nblk, Q_BLOCK, ATT_DK), 3, 0)
    o = lax.map(lambda qblk: _diff_attend(qblk, k_all, v_all, lam), qb)
    o = o.transpose(1, 0, 3, 2, 4).reshape(B, L, ATT_HEADS, ATT_DV)
    y = _diff_head_out(o, z, subln_g, lambda_init, w_out)
    if not with_ctx_queries:
        return y, None
    qc = (hc @ w_q).reshape(B, Lc, ATT_HEADS, 2, ATT_DK)
    oc = _diff_attend(to_bhmld(qc), to_bhmld(kc), vc.transpose(0, 2, 1, 3), lam).transpose(0, 2, 1, 3)
    yc = _diff_head_out(oc, hc @ w_z, subln_g, lambda_init, w_out)
    return y, yc


def conformer_conv_mixer(h, w_in, dw_w, dw_b, ln_g, ln_b, w_out):
    a, g, z = jnp.split(h @ w_in, 3, axis=-1)
    y = a * jax.nn.sigmoid(g)
    y = depthwise_conv_centered(y, dw_w, dw_b)
    y = jax.nn.silu(layer_norm(y, ln_g, ln_b))
    return (y * jax.nn.silu(z)) @ w_out


def short_conv_mixer(h, w_in, conv_w, w_out):
    v, bg, cg, z = jnp.split(h @ w_in, 4, axis=-1)
    y = bg * depthwise_conv_centered(cg * v, conv_w)
    return (y * jax.nn.silu(z)) @ w_out


def multiscale_pool_minus_identity(h):
    L = h.shape[1]
    cs = jnp.cumsum(h.astype(jnp.float32), axis=1)
    cs = jnp.concatenate([jnp.zeros_like(cs[:, :1]), cs], axis=1)
    t = jnp.arange(L)
    outs = []
    for g, w in enumerate(POOL_WINDOWS):
        lo = jnp.clip(t - w // 2, 0, L)
        hi = jnp.clip(t + w // 2, 0, L)
        csg = cs[..., g * POOL_GROUP:(g + 1) * POOL_GROUP]
        s = jnp.take(csg, hi, axis=1) - jnp.take(csg, lo, axis=1)
        cnt = (hi - lo).astype(jnp.float32)[None, :, None]
        outs.append(s / cnt - h[..., g * POOL_GROUP:(g + 1) * POOL_GROUP].astype(jnp.float32))
    return jnp.stack(outs, axis=2).astype(h.dtype)


def pool_mixer(h, w_in, w_group, scale, w_out):
    v, z = jnp.split(h @ w_in, 2, axis=-1)
    p = multiscale_pool_minus_identity(v)
    y = jnp.einsum('blgc,gcd->blgd', p, w_group).reshape(v.shape) * scale
    return (y * jax.nn.silu(z)) @ w_out


def setup_inputs(seed: int = 0) -> dict:
    key = jax.random.key(seed)
    ks = iter(jax.random.split(key, 64))
    nrm = lambda shape, s: jax.random.normal(next(ks), shape, jnp.float32) * s
    nA, nB, nC, nD = (_layers_of_kind(k) for k in range(N_MIXERS))
    D = D_MODEL
    return {
        "x": nrm((BATCH, SEQ, D), 1.0),
        "c": nrm((BATCH, D), 1.0),
        "ctx": nrm((BATCH, CTX_LEN, D), 1.0),
        "c_ctx": nrm((D,), 1.0),
        "mod_w": nrm((DEPTH, D, 3 * D), D ** -0.5),
        "mod_b": nrm((DEPTH, 3 * D), 0.02),
        "ln_g": 1.0 + nrm((DEPTH, D), 0.05),
        "ln_b": nrm((DEPTH, D), 0.02),
        "attn_w_in": nrm((nA, D, 2 * ATT_QK_WIDTH + 2 * ATT_WIDTH), D ** -0.5),
        "attn_w_out": nrm((nA, ATT_WIDTH, D), ATT_WIDTH ** -0.5 * DEEPNORM_BETA),
        "attn_lambda_q1": nrm((nA, ATT_DK), 0.1),
        "attn_lambda_k1": nrm((nA, ATT_DK), 0.1),
        "attn_lambda_q2": nrm((nA, ATT_DK), 0.1),
        "attn_lambda_k2": nrm((nA, ATT_DK), 0.1),
        "attn_subln_g": 1.0 + nrm((nA, ATT_DV), 0.05),
        "conf_w_in": nrm((nB, D, 3 * CONF_WIDTH), D ** -0.5),
        "conf_dw_w": nrm((nB, CONF_KW, CONF_WIDTH), CONF_KW ** -0.5),
        "conf_dw_b": nrm((nB, CONF_WIDTH), 0.02),
        "conf_ln_g": 1.0 + nrm((nB, CONF_WIDTH), 0.05),
        "conf_ln_b": nrm((nB, CONF_WIDTH), 0.02),
        "conf_w_out": nrm((nB, CONF_WIDTH, D), CONF_WIDTH ** -0.5 * DEEPNORM_BETA),
        "sc_w_in": nrm((nC, D, 4 * SC_WIDTH), D ** -0.5),
        "sc_conv_w": nrm((nC, SC_KW, SC_WIDTH), SC_KW ** -0.5),
        "sc_w_out": nrm((nC, SC_WIDTH, D), SC_WIDTH ** -0.5 * DEEPNORM_BETA),
        "pool_w_in": nrm((nD, D, 2 * POOL_WIDTH), D ** -0.5),
        "pool_w_group": nrm((nD, len(POOL_WINDOWS), POOL_GROUP, POOL_GROUP), POOL_GROUP ** -0.5),
        "pool_scale": 1.0 + nrm((nD, POOL_WIDTH), 0.1),
        "pool_w_out": nrm((nD, POOL_WIDTH, D), POOL_WIDTH ** -0.5 * DEEPNORM_BETA),
    }


def reference(x, c, ctx, c_ctx, mod_w, mod_b, ln_g, ln_b,
              attn_w_in, attn_w_out, attn_lambda_q1, attn_lambda_k1, attn_lambda_q2, attn_lambda_k2, attn_subln_g,
              conf_w_in, conf_dw_w, conf_dw_b, conf_ln_g, conf_ln_b, conf_w_out,
              sc_w_in, sc_conv_w, sc_w_out,
              pool_w_in, pool_w_group, pool_scale, pool_w_out):
    L = x.shape[1]
    cos, sin = axial_rope_tables(L, ATT_DK)
    last_ctx_reader = max(i for i in range(DEPTH) if i % N_MIXERS == 0)
    silu_c = jax.nn.silu(c)
    silu_cc = jax.nn.silu(c_ctx)

    def seq_mixer(kind, j, s):
        if kind == 1:
            return conformer_conv_mixer(s, conf_w_in[j], conf_dw_w[j], conf_dw_b[j], conf_ln_g[j], conf_ln_b[j], conf_w_out[j])
        if kind == 2:
            return short_conv_mixer(s, sc_w_in[j], sc_conv_w[j], sc_w_out[j])
        return pool_mixer(s, pool_w_in[j], pool_w_group[j], pool_scale[j], pool_w_out[j])

    for i in range(DEPTH):
        kind, j = i % N_MIXERS, i // N_MIXERS
        update_ctx = i < last_ctx_reader
        shift, scale, gate = jnp.split(silu_c @ mod_w[i] + mod_b[i], 3, axis=-1)
        h = x * (1 + scale[:, None, :]) + shift[:, None, :]
        hc, cgate, out_c = None, None, None
        if kind == 0 or update_ctx:
            cshift, cscale, cgate = jnp.split(silu_cc @ mod_w[i] + mod_b[i], 3, axis=-1)
            hc = ctx * (1 + cscale) + cshift
        if kind == 0:
            lambda_init = 0.8 - 0.6 * math.exp(-0.3 * i)
            out, out_c = diff_attention_mixer(h, hc, attn_w_in[j], attn_w_out[j], attn_lambda_q1[j], attn_lambda_k1[j],
                                              attn_lambda_q2[j], attn_lambda_k2[j], attn_subln_g[j], lambda_init,
                                              cos, sin, update_ctx)
        else:
            out = seq_mixer(kind, j, h)
            if update_ctx:
                out_c = seq_mixer(kind, j, hc)
        x = layer_norm(DEEPNORM_ALPHA * x + gate[:, None, :] * out, ln_g[i], ln_b[i])
        if update_ctx:
            ctx = layer_norm(DEEPNORM_ALPHA * ctx + cgate * out_c, ln_g[i], ln_b[i])
    return x
```

```python
import functools
import math

import jax
import jax.numpy as jnp
from jax import lax
from jax.experimental import pallas as pl
from jax.experimental.pallas import tpu as pltpu

F32 = jnp.float32
BF16 = jnp.bfloat16

N_MIXERS = 4
GRID_W = 64
ATT_HEADS = 8
ATT_DK = 64
ATT_DV = 2 * ATT_DK
ROPE_BASE = 10000.0
CONF_KW = 31
SC_KW = 3
POOL_WINDOWS = (2, 4, 8, 16)
LN_EPS = 1e-5

SUBLANES = 8
LANES = 128
VMEM_LIMIT_BYTES = 52 * 1024 * 1024

MOD_ROWS = 16
PROJ_TM = 512
ATT_TQ = 256
ATT_TK = 512
MIX_TM = 256
CONV_RB = 32
CONV_CB = 256


def _silu(x):
    return x * jax.nn.sigmoid(x)


def _layer_norm(x, g, b):
    mu = jnp.mean(x, axis=-1, keepdims=True)
    xc = x - mu
    var = jnp.mean(xc * xc, axis=-1, keepdims=True)
    return xc * lax.rsqrt(var + LN_EPS) * g + b


def _params(sem):
    return pltpu.CompilerParams(dimension_semantics=sem, vmem_limit_bytes=VMEM_LIMIT_BYTES)


def _mod_kernel(c_ref, w_ref, b_ref, o_ref):
    s = _silu(c_ref[...]).astype(BF16)
    o_ref[0] = jnp.dot(s, w_ref[0].astype(BF16), preferred_element_type=F32) + b_ref[0]


def _modulation(rows, mod_w, mod_b):
    depth, d, n = mod_w.shape
    tn = 1024
    return pl.pallas_call(
        _mod_kernel,
        grid=(depth, n // tn),
        in_specs=[
            pl.BlockSpec((MOD_ROWS, d), lambda i, j: (0, 0)),
            pl.BlockSpec((1, d, tn), lambda i, j: (i, 0, j)),
            pl.BlockSpec((1, 1, tn), lambda i, j: (i, 0, j)),
        ],
        out_specs=pl.BlockSpec((1, MOD_ROWS, tn), lambda i, j: (i, 0, j)),
        out_shape=jax.ShapeDtypeStruct((depth, MOD_ROWS, n), F32),
        compiler_params=_params(("arbitrary", "arbitrary")),
        name="modulation",
    )(rows, mod_w, mod_b.reshape(depth, 1, n))


def _lambda_kernel(lq_ref, lk_ref, o_ref, *, lambda_init):
    e = jnp.exp(jnp.sum(lq_ref[...] * lk_ref[...], axis=-1, keepdims=True))
    o_ref[...] = e[0:1] - e[1:2] + lambda_init


def _attn_lambda(lq, lk, lambda_init):
    return pl.pallas_call(
        functools.partial(_lambda_kernel, lambda_init=lambda_init),
        out_shape=jax.ShapeDtypeStruct((1, 1), F32),
        name="attn_lambda",
    )(lq, lk)


def _rope(t, cos, sin_next, sin_prev):
    return t * cos + pltpu.roll(t, LANES - 16, 1) * sin_next + pltpu.roll(t, 16, 1) * sin_prev


def _attn_in_kernel(x_ref, shift_ref, scale_ref, cos_ref, sa_ref, sb_ref,
                    wq_ref, wk_ref, wv_ref, wz_ref, q_ref, k_ref, v_ref, z_ref):
    h = (x_ref[0] * (1.0 + scale_ref[0]) + shift_ref[0]).astype(BF16)
    cos, sa, sb = cos_ref[...], sa_ref[...], sb_ref[...]
    q = jnp.dot(h, wq_ref[...], preferred_element_type=F32)
    k = jnp.dot(h, wk_ref[...], preferred_element_type=F32)
    for hd in range(ATT_HEADS):
        sl = slice(hd * LANES, (hd + 1) * LANES)
        q_ref[0, :, sl] = (_rope(q[:, sl], cos, sa, sb) * (ATT_DK ** -0.5)).astype(BF16)
        k_ref[0, :, sl] = _rope(k[:, sl], cos, sa, sb).astype(BF16)
    v_ref[0] = jnp.dot(h, wv_ref[...], preferred_element_type=F32).astype(BF16)
    z_ref[0] = jnp.dot(h, wz_ref[...], preferred_element_type=F32)


def _attn_in(x, shift, scale, cos, sa, sb, wq, wk, wv, wz):
    b, l, d = x.shape
    tm = PROJ_TM
    row = pl.BlockSpec((1, tm, d), lambda bi, i: (bi, i, 0))
    vec = pl.BlockSpec((1, 1, d), lambda bi, i: (bi, 0, 0))
    tab = pl.BlockSpec((tm, LANES), lambda bi, i: (i, 0))
    wsp = pl.BlockSpec((d, d), lambda bi, i: (0, 0))
    return pl.pallas_call(
        _attn_in_kernel,
        grid=(b, l // tm),
        in_specs=[row, vec, vec, tab, tab, tab, wsp, wsp, wsp, wsp],
        out_specs=[row, row, row, row],
        out_shape=[jax.ShapeDtypeStruct((b, l, d), BF16)] * 3 + [jax.ShapeDtypeStruct((b, l, d), F32)],
        compiler_params=_params(("arbitrary", "arbitrary")),
        name="attn_in",
    )(x, shift, scale, cos, sa, sb, wq, wk, wv, wz)


def _ctx_kv_kernel(c_ref, shift_ref, scale_ref, wk_ref, wv_ref, k_ref, v_ref):
    h = (c_ref[0] * (1.0 + scale_ref[...]) + shift_ref[...]).astype(BF16)
    k_ref[0] = jnp.dot(h, wk_ref[...], preferred_element_type=F32).astype(BF16)
    v_ref[0] = jnp.dot(h, wv_ref[...], preferred_element_type=F32).astype(BF16)


def _ctx_kv(ctx, cshift, cscale, wk, wv):
    b, lc, d = ctx.shape
    row = pl.BlockSpec((1, lc, d), lambda bi: (bi, 0, 0))
    vec = pl.BlockSpec((1, d), lambda bi: (0, 0))
    wsp = pl.BlockSpec((d, d), lambda bi: (0, 0))
    return pl.pallas_call(
        _ctx_kv_kernel,
        grid=(b,),
        in_specs=[row, vec, vec, wsp, wsp],
        out_specs=[row, row],
        out_shape=[jax.ShapeDtypeStruct((b, lc, d), BF16)] * 2,
        compiler_params=_params(("arbitrary",)),
        name="ctx_kv",
    )(ctx, cshift, cscale, wk, wv)


def _attn_kernel(lam_ref, q_ref, k_ref, v_ref, kc_ref, vc_ref, z_ref, g_ref, o_ref, *,
                 n_chunks, out_scale):
    tq = q_ref.shape[1]
    q = q_ref[0]
    lane = lax.broadcasted_iota(jnp.int32, q.shape, 1)
    zero = jnp.zeros_like(q)
    qs = jnp.concatenate([jnp.where(lane < ATT_DK, q, zero), jnp.where(lane >= ATT_DK, q, zero)], axis=0)

    def step(kj, vj, carry):
        m, l, acc = carry
        s = lax.dot_general(qs, kj, (((1,), (1,)), ((), ())), preferred_element_type=F32)
        m_new = jnp.maximum(m, jnp.max(s, axis=-1, keepdims=True))
        alpha = jnp.exp(m - m_new)
        p = jnp.exp(s - m_new)
        l = alpha * l + jnp.sum(p, axis=-1, keepdims=True)
        acc = alpha * acc + jnp.dot(p.astype(BF16), vj, preferred_element_type=F32)
        return m_new, l, acc

    def body(j, carry):
        r0 = pl.multiple_of(j * ATT_TK, ATT_TK)
        return step(k_ref[0, pl.ds(r0, ATT_TK), :], v_ref[0, pl.ds(r0, ATT_TK), :], carry)

    init = (jnp.full((2 * tq, 1), -jnp.inf, F32), jnp.zeros((2 * tq, 1), F32),
            jnp.zeros((2 * tq, ATT_DV), F32))
    carry = lax.fori_loop(0, n_chunks, body, init)
    _, l, acc = step(kc_ref[0], vc_ref[0], carry)
    o = acc / l
    o = o[:tq] - lam_ref[0, 0] * o[tq:]
    o = o * lax.rsqrt(jnp.mean(o * o, axis=-1, keepdims=True) + LN_EPS) * g_ref[...]
    o_ref[0] = (o * out_scale * _silu(z_ref[0])).astype(BF16)


def _attention(lam, q, k, v, kc, vc, z, subln_g, lambda_init):
    b, l, d = q.shape
    lc = kc.shape[1]
    tq = ATT_TQ
    qsp = pl.BlockSpec((1, tq, ATT_DV), lambda bi, h, i: (bi, i, h))
    ksp = pl.BlockSpec((1, l, ATT_DV), lambda bi, h, i: (bi, 0, h))
    csp = pl.BlockSpec((1, lc, ATT_DV), lambda bi, h, i: (bi, 0, h))
    return pl.pallas_call(
        functools.partial(_attn_kernel, n_chunks=l // ATT_TK, out_scale=1.0 - lambda_init),
        grid=(b, ATT_HEADS, l // tq),
        in_specs=[
            pl.BlockSpec(memory_space=pltpu.SMEM),
            qsp, ksp, ksp, csp, csp, qsp,
            pl.BlockSpec((1, ATT_DV), lambda bi, h, i: (0, 0)),
        ],
        out_specs=qsp,
        out_shape=jax.ShapeDtypeStruct((b, l, d), BF16),
        compiler_params=_params(("arbitrary", "arbitrary", "arbitrary")),
        name="diff_attention",
    )(lam, q, k, v, kc, vc, z, subln_g)


def _attn_out_kernel(y_ref, x_ref, gate_ref, w_ref, g_ref, b_ref, o_ref, *, alpha):
    out = jnp.dot(y_ref[0], w_ref[...], preferred_element_type=F32)
    o_ref[0] = _layer_norm(alpha * x_ref[0] + gate_ref[0] * out, g_ref[...], b_ref[...])


def _attn_out(y, x, gate, w_out, ln_g, ln_b, alpha):
    b, l, d = x.shape
    tm = PROJ_TM
    row = pl.BlockSpec((1, tm, d), lambda bi, i: (bi, i, 0))
    vec = pl.BlockSpec((1, 1, d), lambda bi, i: (bi, 0, 0))
    par = pl.BlockSpec((1, d), lambda bi, i: (0, 0))
    return pl.pallas_call(
        functools.partial(_attn_out_kernel, alpha=alpha),
        grid=(b, l // tm),
        in_specs=[row, row, vec, pl.BlockSpec((d, d), lambda bi, i: (0, 0)), par, par],
        out_specs=row,
        out_shape=jax.ShapeDtypeStruct((b, l, d), F32),
        compiler_params=_params(("arbitrary", "arbitrary")),
        name="attn_out",
    )(y, x, gate, w_out, ln_g, ln_b)


def _shift_rows(t, s):
    n = t.shape[0]
    return pltpu.roll(t, (-s) % n, 0)


def _mixer_kernel(xp_ref, xc_ref, xn_ref, shift_ref, scale_ref, gate_ref,
                  wh_ref, wc_ref, *rest, kind, halo, seq_len, alpha):
    if kind == 1:
        (dww_ref, dwb_ref, cg_ref, cb_ref, wo_ref, lg_ref, lb_ref, o_ref, ysh_ref, conv_ref) = rest
    elif kind == 2:
        (cw_ref, wo_ref, lg_ref, lb_ref, o_ref) = rest
    else:
        (wg_ref, ps_ref, wo_ref, lg_ref, lb_ref, o_ref) = rest
    tm, d = xc_ref.shape[1], xc_ref.shape[2]
    n = tm + 2 * halo
    i = pl.program_id(1)
    xc = xc_ref[0]
    xe = jnp.concatenate([xp_ref[0], xc, xn_ref[0]], axis=0)
    scale1 = 1.0 + scale_ref[0]
    shift = shift_ref[0]
    he = (xe * scale1 + shift).astype(BF16)
    hc = (xc * scale1 + shift).astype(BF16)
    pos = i * tm - halo + lax.broadcasted_iota(jnp.int32, (n, 1), 0)
    inside = jnp.logical_and(pos >= 0, pos < seq_len)
    wide = jnp.dot(he, wh_ref[...], preferred_element_type=F32)
    ctr = jnp.dot(hc, wc_ref[...], preferred_element_type=F32)

    if kind == 1:
        y = jnp.where(inside, wide[:, :d] * jax.nn.sigmoid(wide[:, d:]), 0.0)
        for r in range(SUBLANES):
            ysh_ref[r] = y if r == 0 else _shift_rows(y, r)
        off0 = halo - (CONF_KW - 1) // 2
        for c0 in range(0, d, CONV_CB):
            cs = slice(c0, c0 + CONV_CB)

            def rows(rb, _, cs=cs):
                r0 = pl.multiple_of(rb * CONV_RB, CONV_RB)
                acc = jnp.broadcast_to(dwb_ref[:, cs], (CONV_RB, CONV_CB))
                for kk in range(CONF_KW):
                    o = off0 + kk
                    tap = ysh_ref[o % SUBLANES, pl.ds(r0 + (o // SUBLANES) * SUBLANES, CONV_RB), cs]
                    acc = acc + dww_ref[kk:kk + 1, cs] * tap
                conv_ref[pl.ds(r0, CONV_RB), cs] = acc
                return 0

            lax.fori_loop(0, tm // CONV_RB, rows, 0)
        t = _silu(_layer_norm(conv_ref[...], cg_ref[...], cb_ref[...]))
        mixed = t * _silu(ctr)
    elif kind == 2:
        u = jnp.where(inside, wide[:, d:] * wide[:, :d], 0.0)
        conv = (cw_ref[0:1, :] * _shift_rows(u, -1) + cw_ref[1:2, :] * u + cw_ref[2:3, :] * _shift_rows(u, 1))
        mixed = ctr[:, :d] * conv[halo:halo + tm] * _silu(ctr[:, d:])
    else:
        v = jnp.where(inside, wide, 0.0)
        posc = pos[halo:halo + tm]
        groups = []
        gw = d // len(POOL_WINDOWS)
        for gi, w in enumerate(POOL_WINDOWS):
            vg = v[:, gi * gw:(gi + 1) * gw]
            win = _shift_rows(vg, -1) + vg
            span = 1
            while 2 * span < w:
                win = _shift_rows(win, -span) + _shift_rows(win, span)
                span *= 2
            cnt = (jnp.minimum(posc + w // 2, seq_len) - jnp.maximum(posc - w // 2, 0)).astype(F32)
            p = win[halo:halo + tm] / cnt - vg[halo:halo + tm]
            groups.append(jnp.dot(p.astype(BF16), wg_ref[gi], preferred_element_type=F32))
        mixed = jnp.concatenate(groups, axis=-1) * ps_ref[...] * _silu(ctr)

    out = jnp.dot(mixed.astype(BF16), wo_ref[...], preferred_element_type=F32)
    o_ref[0] = _layer_norm(alpha * xc + gate_ref[0] * out, lg_ref[...], lb_ref[...])


def _mixer_layer(kind, x, shift, scale, gate, w_halo, w_ctr, extras, w_out, ln_g, ln_b, alpha):
    b, l, d = x.shape
    tm = MIX_TM
    halo = 2 * SUBLANES if kind == 1 else SUBLANES
    nh = tm // halo
    last = l // halo - 1
    row = pl.BlockSpec((1, tm, d), lambda bi, i: (bi, i, 0))
    prev = pl.BlockSpec((1, halo, d), lambda bi, i: (bi, jnp.maximum(i * nh - 1, 0), 0))
    nxt = pl.BlockSpec((1, halo, d), lambda bi, i: (bi, jnp.minimum((i + 1) * nh, last), 0))
    vec = pl.BlockSpec((1, 1, d), lambda bi, i: (bi, 0, 0))

    def whole(a):
        nd = a.ndim
        return pl.BlockSpec(a.shape, lambda bi, i: (0,) * nd)

    scratch = []
    if kind == 1:
        n = tm + 2 * halo
        scratch = [pltpu.VMEM((SUBLANES, n, d), F32), pltpu.VMEM((tm, d), F32)]
    consts = [w_halo, w_ctr, *extras, w_out, ln_g, ln_b]
    return pl.pallas_call(
        functools.partial(_mixer_kernel, kind=kind, halo=halo, seq_len=l, alpha=alpha),
        grid=(b, l // tm),
        in_specs=[prev, row, nxt, vec, vec, vec] + [whole(a) for a in consts],
        out_specs=row,
        out_shape=jax.ShapeDtypeStruct((b, l, d), F32),
        scratch_shapes=scratch,
        compiler_params=_params(("arbitrary", "arbitrary")),
        name=f"mixer_kind{kind}",
    )(x, x, x, shift, scale, gate, *consts)


def _rope_tables(length):
    half = ATT_DK // 2
    inv = 1.0 / (ROPE_BASE ** (jnp.arange(0, half, 2, dtype=F32) / half))
    t = jnp.arange(length, dtype=jnp.int32)
    row, col = jnp.divmod(t, GRID_W)
    ar = row.astype(F32)[:, None] * inv[None, :]
    ac = col.astype(F32)[:, None] * inv[None, :]
    ang = jnp.concatenate([ar, ar, ac, ac] * 2, axis=-1)
    cos, sin = jnp.cos(ang), jnp.sin(ang)
    even_chunk = ((jnp.arange(LANES) // 16) % 2 == 0)[None, :]
    sin_next = jnp.where(even_chunk, -sin, 0.0)
    sin_prev = jnp.where(even_chunk, 0.0, sin)
    return cos, sin_next, sin_prev


def kernel(x, c, ctx, c_ctx, mod_w, mod_b, ln_g, ln_b, attn_w_in, attn_w_out, attn_lambda_q1, attn_lambda_k1, attn_lambda_q2, attn_lambda_k2, attn_subln_g, conf_w_in, conf_dw_w, conf_dw_b, conf_ln_g, conf_ln_b, conf_w_out, sc_w_in, sc_conv_w, sc_w_out, pool_w_in, pool_w_group, pool_scale, pool_w_out):
    b, l, d = x.shape
    depth = mod_w.shape[0]
    alpha = (2 * depth) ** 0.25
    last_ctx_reader = max(i for i in range(depth) if i % N_MIXERS == 0)

    rows = jnp.concatenate([c, c_ctx[None, :], jnp.zeros((MOD_ROWS - b - 1, d), F32)], axis=0)
    mods = _modulation(rows, mod_w, mod_b)

    for i in range(depth):
        kind, j = i % N_MIXERS, i // N_MIXERS
        update_ctx = i < last_ctx_reader
        shift = mods[i, :b, 0:d].reshape(b, 1, d)
        scale = mods[i, :b, d:2 * d].reshape(b, 1, d)
        gate = mods[i, :b, 2 * d:].reshape(b, 1, d)
        g_i, b_i = ln_g[i][None, :], ln_b[i][None, :]
        if kind == 0:
            assert not update_ctx, "context stream update is not needed at this depth"
            lambda_init = 0.8 - 0.6 * math.exp(-0.3 * i)
            cshift = mods[i, b:b + 1, 0:d]
            cscale = mods[i, b:b + 1, d:2 * d]
            w = attn_w_in[j].astype(BF16)
            wq, wk, wv, wz = (w[:, n * d:(n + 1) * d] for n in range(4))
            cos, sa, sb = _rope_tables(l)
            q, k, v, z = _attn_in(x, shift, scale, cos, sa, sb, wq, wk, wv, wz)
            kc, vc = _ctx_kv(ctx, cshift, cscale, wk, wv)
            lam = _attn_lambda(jnp.stack([attn_lambda_q1[j], attn_lambda_q2[j]]),
                               jnp.stack([attn_lambda_k1[j], attn_lambda_k2[j]]), lambda_init)
            y = _attention(lam, q, k, v, kc, vc, z, attn_subln_g[j][None, :], lambda_init)
            x = _attn_out(y, x, gate, attn_w_out[j].astype(BF16), g_i, b_i, alpha)
        elif kind == 1:
            assert not update_ctx
            w = conf_w_in[j].astype(BF16)
            extras = [conf_dw_w[j], conf_dw_b[j][None, :], conf_ln_g[j][None, :], conf_ln_b[j][None, :]]
            x = _mixer_layer(1, x, shift, scale, gate, w[:, :2 * d], w[:, 2 * d:], extras,
                             conf_w_out[j].astype(BF16), g_i, b_i, alpha)
        elif kind == 2:
            assert not update_ctx
            w = sc_w_in[j].astype(BF16)
            w_halo = jnp.concatenate([w[:, 0:d], w[:, 2 * d:3 * d]], axis=1)
            w_ctr = jnp.concatenate([w[:, d:2 * d], w[:, 3 * d:]], axis=1)
            x = _mixer_layer(2, x, shift, scale, gate, w_halo, w_ctr, [sc_conv_w[j]],
                             sc_w_out[j].astype(BF16), g_i, b_i, alpha)
        else:
            assert not update_ctx
            w = pool_w_in[j].astype(BF16)
            extras = [pool_w_group[j].astype(BF16), pool_scale[j][None, :]]
            x = _mixer_layer(3, x, shift, scale, gate, w[:, :d], w[:, d:], extras,
                             pool_w_out[j].astype(BF16), g_i, b_i, alpha)
    return x
```
